```python
import math
import jax, jax.numpy as jnp
from jax import lax
import numpy as np

D_MODEL = 2048
BATCH = 2
SEQ = 4096
DEPTH = 1

DA_HEADS = 8
DA_HEAD_DIM = 64
DA_V_DIM = 2 * DA_HEAD_DIM
DA_QK = 2 * DA_HEADS * DA_HEAD_DIM
DA_V = DA_HEADS * DA_V_DIM
GDN_HEADS = 8
GDN_K_DIM = 128
GDN_V_DIM = 128
GDN_QK = GDN_HEADS * GDN_K_DIM
GDN_V = GDN_HEADS * GDN_V_DIM
GDN_CONV_CH = 2 * GDN_QK + GDN_V
CONV_WIDTH = 4
CHUNK = 64
MIX_WIDTH = DA_V + GDN_V
IN_COLS = 2 * DA_QK + DA_V + GDN_CONV_CH + GDN_V + 2 * GDN_HEADS
D_FF = 4 * D_MODEL
Q_BLOCK = 128
ROPE_THETA = 10000.0
EPS = 1e-6
NEG_INF = -1e30

kernel_name = "hymba_diffattn_gdn_sqrelu_sandwich"


def rmsnorm(x, w):
    xf = x.astype(jnp.float32)
    y = xf * lax.rsqrt(jnp.mean(xf * xf, axis=-1, keepdims=True) + EPS)
    return (y * w.astype(jnp.float32)).astype(x.dtype)


def rotary(x):
    s, d = x.shape[1], x.shape[-1]
    inv_freq = ROPE_THETA ** (-jnp.arange(0, d, 2, dtype=jnp.float32) / d)
    ang = jnp.arange(s, dtype=jnp.float32)[:, None] * inv_freq[None, :]
    cos = jnp.cos(ang)[None, :, None, :]
    sin = jnp.sin(ang)[None, :, None, :]
    xf = x.astype(jnp.float32)
    x1, x2 = xf[..., : d // 2], xf[..., d // 2:]
    out = jnp.concatenate([x1 * cos - x2 * sin, x2 * cos + x1 * sin], axis=-1)
    return out.astype(x.dtype)


def diff_attention(q, k, v, lam, subln_w, lambda_init):
    b, s, h2, d = q.shape
    n_blocks = s // Q_BLOCK
    scale = d ** -0.5
    kf = k.astype(jnp.float32)
    vf = v.astype(jnp.float32)
    qb = jnp.moveaxis(q.astype(jnp.float32).reshape(b, n_blocks, Q_BLOCK, h2, d), 1, 0)
    key_pos = jnp.arange(s)

    def block(args):
        q_blk, blk = args
        scores = jnp.einsum('bqhd,bkhd->bhqk', q_blk, kf) * scale
        q_pos = blk * Q_BLOCK + jnp.arange(Q_BLOCK)
        causal = key_pos[None, :] <= q_pos[:, None]
        scores = jnp.where(causal, scores, NEG_INF)
        p = jax.nn.softmax(scores, axis=-1).reshape(b, h2 // 2, 2, Q_BLOCK, s)
        diff = p[:, :, 0] - lam * p[:, :, 1]
        return jnp.einsum('bhqk,bkhe->bqhe', diff, vf)

    out = lax.map(block, (qb, jnp.arange(n_blocks)))
    out = jnp.moveaxis(out, 0, 1).reshape(b, s, h2 // 2, 2 * d)
    out = rmsnorm(out, subln_w) * (1.0 - lambda_init)
    return out.astype(q.dtype).reshape(b, s, (h2 // 2) * 2 * d)


def causal_depthwise_conv(x, w):
    c = x.shape[-1]
    return lax.conv_general_dilated(
        x, w[:, None, :].astype(x.dtype), window_strides=(1,),
        padding=[(CONV_WIDTH - 1, 0)], dimension_numbers=('NWC', 'WIO', 'NWC'),
        feature_group_count=c)


def l2norm(x):
    return x * lax.rsqrt(jnp.sum(x * x, axis=-1, keepdims=True) + EPS)


def chunked_gated_delta_rule(q, k, v, beta, g):
    b, s, h, dk = q.shape
    dv = v.shape[-1]
    n = s // CHUNK

    def to_chunks(t):
        t = jnp.moveaxis(t, 2, 1)
        t = t.reshape((b, h, n, CHUNK) + t.shape[3:])
        return jnp.moveaxis(t, 2, 0)

    q, k, v, beta, g = (to_chunks(t) for t in (q, k, v, beta, g))
    g = jnp.cumsum(g, axis=-1)
    idx = jnp.arange(CHUNK)
    incl = idx[:, None] >= idx[None, :]
    strict = idx[:, None] > idx[None, :]
    gdiff = g[..., :, None] - g[..., None, :]
    decay = jnp.where(incl, jnp.exp(jnp.where(incl, gdiff, 0.0)), 0.0)
    k_beta = k * beta[..., None]
    v_beta = v * beta[..., None]
    lower = jnp.where(strict, jnp.einsum('nbhid,nbhjd->nbhij', k_beta, k) * decay, 0.0)
    a = lower + jnp.eye(CHUNK, dtype=jnp.float32)
    u = lax.linalg.triangular_solve(a, v_beta, left_side=True, lower=True, unit_diagonal=True)
    w = lax.linalg.triangular_solve(a, k_beta * jnp.exp(g)[..., None], left_side=True,
                                    lower=True, unit_diagonal=True)
    qk = jnp.einsum('nbhid,nbhjd->nbhij', q, k) * decay
    q_decayed = q * jnp.exp(g)[..., None]
    k_to_end = k * jnp.exp(g[..., -1:] - g)[..., None]
    g_last = jnp.exp(g[..., -1])

    def step(state, xs):
        q_d, qk_c, u_c, w_c, k_e, gl = xs
        v_new = u_c - jnp.einsum('bhcd,bhde->bhce', w_c, state)
        o = (jnp.einsum('bhcd,bhde->bhce', q_d, state)
             + jnp.einsum('bhij,bhje->bhie', qk_c, v_new))
        state = state * gl[..., None, None] + jnp.einsum('bhcd,bhce->bhde', k_e, v_new)
        return state, o

    state0 = jnp.zeros((b, h, dk, dv), jnp.float32)
    _, o = lax.scan(step, state0, (q_decayed, qk, u, w, k_to_end, g_last))
    o = jnp.moveaxis(o, 0, 2).reshape(b, h, s, dv)
    return jnp.moveaxis(o, 1, 2)


def setup_inputs(seed: int = 0) -> dict:
    key = jax.random.key(seed)
    ks = jax.random.split(key, 20)
    f32 = jnp.float32

    def gain(k, n):
        return 1.0 + 0.05 * jax.random.normal(k, (DEPTH, n), f32)

    x = jax.random.normal(ks[0], (BATCH, SEQ, D_MODEL), f32)
    w_in = jax.random.normal(ks[1], (DEPTH, D_MODEL, IN_COLS), f32) * D_MODEL ** -0.5
    conv_w = jax.random.normal(ks[2], (DEPTH, CONV_WIDTH, GDN_CONV_CH), f32) * CONV_WIDTH ** -0.5
    a_log = jnp.log(jax.random.uniform(ks[3], (DEPTH, GDN_HEADS), f32, 1.0, 16.0))
    dt = jnp.exp(jax.random.uniform(ks[4], (DEPTH, GDN_HEADS), f32, math.log(1e-3), math.log(1e-1)))
    dt_bias = dt + jnp.log(-jnp.expm1(-dt))
    gdn_norm_w = gain(ks[5], GDN_V_DIM)
    lambda_q1 = 0.1 * jax.random.normal(ks[6], (DEPTH, DA_HEAD_DIM), f32)
    lambda_k1 = 0.1 * jax.random.normal(ks[7], (DEPTH, DA_HEAD_DIM), f32)
    lambda_q2 = 0.1 * jax.random.normal(ks[8], (DEPTH, DA_HEAD_DIM), f32)
    lambda_k2 = 0.1 * jax.random.normal(ks[9], (DEPTH, DA_HEAD_DIM), f32)
    da_subln_w = gain(ks[10], DA_V_DIM)
    w_out = jax.random.normal(ks[11], (DEPTH, MIX_WIDTH, D_MODEL), f32) * MIX_WIDTH ** -0.5
    w_up = jax.random.normal(ks[12], (DEPTH, D_MODEL, D_FF), f32) * D_MODEL ** -0.5
    w_down = jax.random.normal(ks[13], (DEPTH, D_FF, D_MODEL), f32) * D_FF ** -0.5
    norm_pre_mix = gain(ks[14], D_MODEL)
    norm_post_mix = gain(ks[15], D_MODEL)
    norm_pre_mlp = gain(ks[16], D_MODEL)
    norm_post_mlp = gain(ks[17], D_MODEL)
    return {"x": x, "w_in": w_in, "conv_w": conv_w, "a_log": a_log, "dt_bias": dt_bias,
            "gdn_norm_w": gdn_norm_w, "lambda_q1": lambda_q1, "lambda_k1": lambda_k1,
            "lambda_q2": lambda_q2, "lambda_k2": lambda_k2, "da_subln_w": da_subln_w,
            "w_out": w_out, "w_up": w_up, "w_down": w_down,
            "norm_pre_mix": norm_pre_mix, "norm_post_mix": norm_post_mix,
            "norm_pre_mlp": norm_pre_mlp, "norm_post_mlp": norm_post_mlp}


def reference(x, w_in, conv_w, a_log, dt_bias, gdn_norm_w, lambda_q1, lambda_k1,
              lambda_q2, lambda_k2, da_subln_w, w_out, w_up, w_down,
              norm_pre_mix, norm_post_mix, norm_pre_mlp, norm_post_mlp):
    b, s, _ = x.shape
    split_at = [DA_QK, 2 * DA_QK, 2 * DA_QK + DA_V, 2 * DA_QK + DA_V + GDN_CONV_CH,
                2 * DA_QK + DA_V + GDN_CONV_CH + GDN_V,
                2 * DA_QK + DA_V + GDN_CONV_CH + GDN_V + GDN_HEADS]
    for l in range(DEPTH):
        h = rmsnorm(x, norm_pre_mix[l])
        proj = h @ w_in[l]
        da_q, da_k, da_v, gdn_qkv, gdn_z, gdn_b, gdn_a = jnp.split(proj, split_at, axis=-1)

        lambda_init = 0.8 - 0.6 * math.exp(-0.3 * l)
        lam = (jnp.exp(jnp.sum(lambda_q1[l].astype(jnp.float32) * lambda_k1[l].astype(jnp.float32)))
               - jnp.exp(jnp.sum(lambda_q2[l].astype(jnp.float32) * lambda_k2[l].astype(jnp.float32)))
               + lambda_init)
        dq = rotary(da_q.reshape(b, s, 2 * DA_HEADS, DA_HEAD_DIM))
        dk = rotary(da_k.reshape(b, s, 2 * DA_HEADS, DA_HEAD_DIM))
        dv = da_v.reshape(b, s, DA_HEADS, DA_V_DIM)
        da_out = diff_attention(dq, dk, dv, lam, da_subln_w[l], lambda_init)

        conv = jax.nn.silu(causal_depthwise_conv(gdn_qkv, conv_w[l]))
        gq, gk, gv = jnp.split(conv, [GDN_QK, 2 * GDN_QK], axis=-1)
        gq = l2norm(gq.reshape(b, s, GDN_HEADS, GDN_K_DIM).astype(jnp.float32)) * GDN_K_DIM ** -0.5
        gk = l2norm(gk.reshape(b, s, GDN_HEADS, GDN_K_DIM).astype(jnp.float32))
        gv = gv.reshape(b, s, GDN_HEADS, GDN_V_DIM).astype(jnp.float32)
        beta = jax.nn.sigmoid(gdn_b.astype(jnp.float32))
        g = -jnp.exp(a_log[l].astype(jnp.float32)) * jax.nn.softplus(
            gdn_a.astype(jnp.float32) + dt_bias[l].astype(jnp.float32))
        go = chunked_gated_delta_rule(gq, gk, gv, beta, g)
        z = gdn_z.reshape(b, s, GDN_HEADS, GDN_V_DIM).astype(jnp.float32)
        gdn_out = (rmsnorm(go, gdn_norm_w[l]) * jax.nn.silu(z)).astype(x.dtype).reshape(b, s, GDN_V)

        mix = jnp.concatenate([da_out, gdn_out], axis=-1) @ w_out[l]
        x = x + rmsnorm(mix, norm_post_mix[l])

        h = rmsnorm(x, norm_pre_mlp[l])
        y = jnp.square(jax.nn.relu(h @ w_up[l])) @ w_down[l]
        x = x + rmsnorm(y, norm_post_mlp[l])
    return x
```

```python
import functools
import math

import jax
import jax.numpy as jnp
from jax import lax
from jax.experimental import pallas as pl
from jax.experimental.pallas import tpu as pltpu

F32 = jnp.float32
BF16 = jnp.bfloat16

EPS = 1e-6
NEG_INF = -1e30
ROPE_THETA = 10000.0
LANES = 128
DA_HEADS = 8
DA_HEAD_DIM = 64
GDN_HEADS = 8
GDN_DIM = 128
CHUNK = 64
VMEM_LIMIT = 56 * 1024 * 1024


def _cparams(sem):
    return pltpu.CompilerParams(dimension_semantics=sem, vmem_limit_bytes=VMEM_LIMIT)


def _dot(a, b):
    return jnp.dot(a, b, preferred_element_type=F32)


def _dot_nt(a, b):
    return lax.dot_general(a, b, (((1,), (1,)), ((), ())), preferred_element_type=F32)


def _dot_tn(a, b):
    return lax.dot_general(a, b, (((0,), (0,)), ((), ())), preferred_element_type=F32)


def _sigmoid(x):
    return 1.0 / (1.0 + jnp.exp(-x))


def _inproj_kernel(x_ref, nw_ref, w_ref, wba_ref, cos_ref, sin_ref, proj_ref, ba_ref, h_ref,
                   *, n_q_tiles, n_rot_tiles):
    j = pl.program_id(1)

    @pl.when(j == 0)
    def _():
        x = x_ref[...]
        ms = jnp.mean(x * x, axis=-1, keepdims=True)
        hb = (x * lax.rsqrt(ms + EPS) * nw_ref[...]).astype(BF16)
        h_ref[...] = hb
        ba_ref[...] = _dot(hb, wba_ref[...])

    acc = _dot(h_ref[...], w_ref[...])

    @pl.when(j < n_rot_tiles)
    def _():
        scale = jnp.where(j < n_q_tiles, DA_HEAD_DIM ** -0.5, 1.0).astype(F32)
        cos = cos_ref[...] * scale
        sin = sin_ref[...] * scale
        lane = lax.broadcasted_iota(jnp.int32, cos.shape, 1)
        first_half = (lane % DA_HEAD_DIM) < (DA_HEAD_DIM // 2)
        for c in range(acc.shape[1] // LANES):
            a = acc[:, c * LANES:(c + 1) * LANES]
            partner = jnp.where(first_half,
                                pltpu.roll(a, LANES - DA_HEAD_DIM // 2, 1),
                                pltpu.roll(a, DA_HEAD_DIM // 2, 1))
            proj_ref[:, c * LANES:(c + 1) * LANES] = (a * cos + partner * sin).astype(BF16)

    @pl.when(j >= n_rot_tiles)
    def _():
        proj_ref[...] = acc.astype(BF16)


def _inproj(x, norm_w, w_main, w_ba, cos, sin, *, seq, tm=1024, tn=512):
    m, d = x.shape
    n = w_main.shape[1]
    qk_cols = 2 * DA_HEADS * DA_HEAD_DIM
    kern = functools.partial(_inproj_kernel, n_q_tiles=qk_cols // tn, n_rot_tiles=2 * qk_cols // tn)
    seq_tiles = seq // tm
    return pl.pallas_call(
        kern,
        grid=(m // tm, n // tn),
        in_specs=[
            pl.BlockSpec((tm, d), lambda i, j: (i, 0)),
            pl.BlockSpec((1, d), lambda i, j: (0, 0)),
            pl.BlockSpec((d, tn), lambda i, j: (0, j)),
            pl.BlockSpec((d, LANES), lambda i, j: (0, 0)),
            pl.BlockSpec((tm, LANES), lambda i, j: (i % seq_tiles, 0)),
            pl.BlockSpec((tm, LANES), lambda i, j: (i % seq_tiles, 0)),
        ],
        out_specs=[
            pl.BlockSpec((tm, tn), lambda i, j: (i, j)),
            pl.BlockSpec((tm, LANES), lambda i, j: (i, 0)),
        ],
        out_shape=[jax.ShapeDtypeStruct((m, n), BF16), jax.ShapeDtypeStruct((m, LANES), F32)],
        scratch_shapes=[pltpu.VMEM((tm, d), BF16)],
        compiler_params=_cparams(("parallel", "arbitrary")),
        name="inproj",
    )(x, norm_w, w_main, w_ba, cos, sin)


def _attn_kernel(q_ref, k_ref, v_ref, lam_ref, subw_ref, o_ref, m_ref, l_ref, acc_ref,
                 *, tq, lambda_init):
    qi = pl.program_id(2)
    q = q_ref[...]
    lane = lax.broadcasted_iota(jnp.int32, q.shape, 1)
    zero = jnp.zeros_like(q)
    qs = jnp.concatenate([jnp.where(lane < DA_HEAD_DIM, q, zero),
                          jnp.where(lane >= DA_HEAD_DIM, q, zero)], axis=0)

    m_ref[...] = jnp.full(m_ref.shape, NEG_INF, F32)
    l_ref[...] = jnp.zeros(l_ref.shape, F32)
    acc_ref[...] = jnp.zeros(acc_ref.shape, F32)

    def step(j, masked):
        k = k_ref[pl.ds(j * tq, tq), :]
        v = v_ref[pl.ds(j * tq, tq), :]
        s = _dot_nt(qs, k)
        if masked:
            row = lax.broadcasted_iota(jnp.int32, s.shape, 0) % tq
            col = lax.broadcasted_iota(jnp.int32, s.shape, 1)
            s = jnp.where(col <= row, s, NEG_INF)
        m_old = m_ref[...]
        m_new = jnp.maximum(m_old, jnp.max(s, axis=1, keepdims=True))
        alpha = jnp.exp(m_old - m_new)
        p = jnp.exp(s - m_new)
        l_ref[...] = alpha * l_ref[...] + jnp.sum(p, axis=1, keepdims=True)
        acc_ref[...] = alpha * acc_ref[...] + _dot(p.astype(BF16), v)
        m_ref[...] = m_new

    def body(j, carry):
        step(j, False)
        return carry

    lax.fori_loop(0, qi, body, 0)
    step(qi, True)

    lam_v = lam_ref[...]
    lam = (jnp.exp(jnp.sum(lam_v[0:1] * lam_v[1:2], axis=-1, keepdims=True))
           - jnp.exp(jnp.sum(lam_v[2:3] * lam_v[3:4], axis=-1, keepdims=True)) + lambda_init)
    o = acc_ref[...] / l_ref[...]
    o = o[:tq] - lam * o[tq:]
    ms = jnp.mean(o * o, axis=-1, keepdims=True)
    o_ref[...] = ((o * lax.rsqrt(ms + EPS) * subw_ref[...]) * (1.0 - lambda_init)).astype(o_ref.dtype)


def _attention(proj, lam_vecs, subln_w, *, batch, seq, lambda_init, tq=512):
    m = proj.shape[0]
    nq = seq // tq
    k_blk0 = DA_HEADS
    v_blk0 = 2 * DA_HEADS
    kern = functools.partial(_attn_kernel, tq=tq, lambda_init=lambda_init)
    return pl.pallas_call(
        kern,
        grid=(batch, DA_HEADS, nq),
        in_specs=[
            pl.BlockSpec((tq, LANES), lambda b, h, i: (b * nq + i, h)),
            pl.BlockSpec((seq, LANES), lambda b, h, i: (b, k_blk0 + h)),
            pl.BlockSpec((seq, LANES), lambda b, h, i: (b, v_blk0 + h)),
            pl.BlockSpec(lam_vecs.shape, lambda b, h, i: (0, 0)),
            pl.BlockSpec((1, LANES), lambda b, h, i: (0, 0)),
        ],
        out_specs=pl.BlockSpec((tq, LANES), lambda b, h, i: (b * nq + i, h)),
        out_shape=jax.ShapeDtypeStruct((m, DA_HEADS * LANES), BF16),
        scratch_shapes=[pltpu.VMEM((2 * tq, 1), F32), pltpu.VMEM((2 * tq, 1), F32),
                        pltpu.VMEM((2 * tq, LANES), F32)],
        compiler_params=_cparams(("parallel", "parallel", "arbitrary")),
        name="diff_attention",
    )(proj, proj, proj, lam_vecs, subln_w)


def _gdn_prep_kernel(q_ref, k_ref, v_ref, hq_ref, hk_ref, hv_ref, ba_ref, cwq_ref, cwk_ref, cwv_ref,
                     alog_ref, dtb_ref, wq_out, u_out, ke_out, qk_out, gl_out, *, rows, blocks_per_seq):
    r = pl.program_id(0)
    h = pl.program_id(1)
    first = (r % blocks_per_seq) == 0
    n_chunks = rows // CHUNK

    def conv_silu(x_ref, halo_ref, cw_ref):
        halo = jnp.where(first, 0.0, halo_ref[...].astype(F32))
        xx = jnp.concatenate([halo, x_ref[...].astype(F32)], axis=0)
        cw = cw_ref[...]
        taps = cw.shape[0]
        y = xx[8:] * cw[taps - 1:taps]
        for s in range(1, taps):
            y = y + pltpu.roll(xx, s, 0)[8:] * cw[taps - 1 - s:taps - s]
        return y * _sigmoid(y)

    q = conv_silu(q_ref, hq_ref, cwq_ref)
    k = conv_silu(k_ref, hk_ref, cwk_ref)
    v = conv_silu(v_ref, hv_ref, cwv_ref)
    qn = q * lax.rsqrt(jnp.sum(q * q, axis=-1, keepdims=True) + EPS) * (GDN_DIM ** -0.5)
    kn = k * lax.rsqrt(jnp.sum(k * k, axis=-1, keepdims=True) + EPS)

    ba = ba_ref[...]
    lane = lax.broadcasted_iota(jnp.int32, ba.shape, 1)
    beta = jnp.sum(jnp.where(lane == h, _sigmoid(ba), 0.0), axis=-1, keepdims=True)
    g_all = -jnp.exp(alog_ref[...]) * jax.nn.softplus(ba + dtb_ref[...])
    g = jnp.sum(jnp.where(lane == GDN_HEADS + h, g_all, 0.0), axis=-1, keepdims=True)

    ri = lax.broadcasted_iota(jnp.int32, (rows, rows), 0)
    ci = lax.broadcasted_iota(jnp.int32, (rows, rows), 1)
    same = (ri // CHUNK) == (ci // CHUNK)
    incl = same & (ri >= ci)
    strict = same & (ri > ci)

    def split(x):
        hi = x.astype(BF16)
        return hi, (x - hi.astype(F32)).astype(BF16)

    g_hi, g_lo = split(jnp.broadcast_to(g, (rows, LANES)))
    incl_b = incl.astype(BF16)
    same_b = same.astype(BF16)
    gc = _dot(incl_b, g_hi) + _dot(incl_b, g_lo)
    gtot = _dot(same_b, g_hi) + _dot(same_b, g_lo)
    gm_hi, gm_lo = split(jnp.where(same & (ri <= ci), g, 0.0))
    ones8 = jnp.ones((8, rows), BF16)
    gc_row = (_dot(ones8, gm_hi) + _dot(ones8, gm_lo))[0:1]
    gdiff = gc[:, 0:1] - gc_row
    decay = jnp.where(incl, jnp.exp(jnp.where(incl, gdiff, 0.0)), 0.0)

    kb = kn * beta
    vb = v * beta
    knb = kn.astype(BF16)
    lower = jnp.where(strict, _dot_nt(kb.astype(BF16), knb) * decay, 0.0)
    qk = _dot_nt(qn.astype(BF16), knb) * decay

    p = -lower
    xs = p
    for _ in range(int(math.log2(CHUNK)) - 1):
        pb = p.astype(BF16)
        p = _dot(pb, pb)
        xs = xs + p + _dot(xs.astype(BF16), p.astype(BF16))

    eg = jnp.exp(gc)
    rhs = jnp.concatenate([vb, kb * eg], axis=1)
    uw = rhs + _dot(xs.astype(BF16), rhs.astype(BF16))
    u = uw[:, :GDN_DIM]
    w = uw[:, GDN_DIM:]
    qd = qn * eg
    ke = kn * jnp.exp(gtot - gc)
    gl = jnp.exp(gtot)

    sel = ((lax.broadcasted_iota(jnp.int32, (rows, CHUNK), 0) % CHUNK)
           == lax.broadcasted_iota(jnp.int32, (rows, CHUNK), 1)).astype(BF16)
    qk_c = _dot(qk.astype(BF16), sel)

    for c in range(n_chunks):
        sl = slice(c * CHUNK, (c + 1) * CHUNK)
        wq_out[0, c, 0:CHUNK, :] = w[sl].astype(BF16)
        wq_out[0, c, CHUNK:2 * CHUNK, :] = qd[sl].astype(BF16)
        u_out[0, c] = u[sl].astype(BF16)
        ke_out[0, c] = ke[sl].astype(BF16)
        qk_out[0, c] = qk_c[sl].astype(BF16)
        gl_out[0, c] = gl[c * CHUNK:c * CHUNK + 1, :]


def _gdn_prep(proj, ba, conv_w, alog_row, dtb_row, *, seq, col0, rows=256):
    m = proj.shape[0]
    nblk = m // rows
    n_chunks_total = m // CHUNK
    cpb = rows // CHUNK
    qb, kb_, vb_ = col0 // LANES, col0 // LANES + GDN_HEADS, col0 // LANES + 2 * GDN_HEADS
    halo = 8
    hpb = rows // halo

    def halo_spec(cb):
        return pl.BlockSpec((halo, LANES), lambda r, h: (jnp.maximum(r * hpb - 1, 0), cb + h))

    taps = conv_w.shape[0]
    kern = functools.partial(_gdn_prep_kernel, rows=rows, blocks_per_seq=seq // rows)
    return pl.pallas_call(
        kern,
        grid=(nblk, GDN_HEADS),
        in_specs=[
            pl.BlockSpec((rows, LANES), lambda r, h: (r, qb + h)),
            pl.BlockSpec((rows, LANES), lambda r, h: (r, kb_ + h)),
            pl.BlockSpec((rows, LANES), lambda r, h: (r, vb_ + h)),
            halo_spec(qb), halo_spec(kb_), halo_spec(vb_),
            pl.BlockSpec((rows, LANES), lambda r, h: (r, 0)),
            pl.BlockSpec((taps, LANES), lambda r, h: (0, h)),
            pl.BlockSpec((taps, LANES), lambda r, h: (0, GDN_HEADS + h)),
            pl.BlockSpec((taps, LANES), lambda r, h: (0, 2 * GDN_HEADS + h)),
            pl.BlockSpec((1, LANES), lambda r, h: (0, 0)),
            pl.BlockSpec((1, LANES), lambda r, h: (0, 0)),
        ],
        out_specs=[
            pl.BlockSpec((1, cpb, 2 * CHUNK, GDN_DIM), lambda r, h: (h, r, 0, 0)),
            pl.BlockSpec((1, cpb, CHUNK, GDN_DIM), lambda r, h: (h, r, 0, 0)),
            pl.BlockSpec((1, cpb, CHUNK, GDN_DIM), lambda r, h: (h, r, 0, 0)),
            pl.BlockSpec((1, cpb, CHUNK, CHUNK), lambda r, h: (h, r, 0, 0)),
            pl.BlockSpec((1, cpb, 1, LANES), lambda r, h: (h, r, 0, 0)),
        ],
        out_shape=[
            jax.ShapeDtypeStruct((GDN_HEADS, n_chunks_total, 2 * CHUNK, GDN_DIM), BF16),
            jax.ShapeDtypeStruct((GDN_HEADS, n_chunks_total, CHUNK, GDN_DIM), BF16),
            jax.ShapeDtypeStruct((GDN_HEADS, n_chunks_total, CHUNK, GDN_DIM), BF16),
            jax.ShapeDtypeStruct((GDN_HEADS, n_chunks_total, CHUNK, CHUNK), BF16),
            jax.ShapeDtypeStruct((GDN_HEADS, n_chunks_total, 1, LANES), F32),
        ],
        compiler_params=_cparams(("parallel", "parallel")),
        name="gdn_prep",
    )(proj, proj, proj, proj, proj, proj, ba, conv_w, conv_w, conv_w, alog_row, dtb_row)


def _gdn_scan_kernel(wq_ref, u_ref, ke_ref, qk_ref, gl_ref, z_ref, nw_ref, o_ref, s_ref, *, chunks):
    @pl.when(pl.program_id(1) == 0)
    def _():
        s_ref[...] = jnp.zeros(s_ref.shape, F32)

    nw = nw_ref[...]

    def body(c, carry):
        row0 = pl.multiple_of(c * CHUNK, CHUNK)
        for h in range(GDN_HEADS):
            state = s_ref[h]
            a1 = _dot(wq_ref[h, c], state.astype(BF16))
            v_new = (u_ref[h, c].astype(F32) - a1[:CHUNK]).astype(BF16)
            o = a1[CHUNK:] + _dot(qk_ref[h, c], v_new)
            s_ref[h] = state * gl_ref[h, c] + _dot_tn(ke_ref[h, c], v_new)
            ms = jnp.mean(o * o, axis=-1, keepdims=True)
            z = z_ref[pl.ds(row0, CHUNK), h * GDN_DIM:(h + 1) * GDN_DIM].astype(F32)
            out = (o * lax.rsqrt(ms + EPS) * nw) * (z * _sigmoid(z))
            o_ref[pl.ds(row0, CHUNK), h * GDN_DIM:(h + 1) * GDN_DIM] = out.astype(o_ref.dtype)
        return carry

    lax.fori_loop(0, chunks, body, 0)


def _gdn_scan(wq, u, ke, qk, gl, proj, norm_w, *, batch, seq, z_col0, chunks=8):
    m = proj.shape[0]
    steps = seq // (chunks * CHUNK)
    rows = chunks * CHUNK
    width = GDN_HEADS * GDN_DIM
    zb = z_col0 // width

    def cspec(a):
        return pl.BlockSpec((GDN_HEADS, chunks) + a.shape[2:], lambda b, t: (0, b * steps + t, 0, 0))

    kern = functools.partial(_gdn_scan_kernel, chunks=chunks)
    return pl.pallas_call(
        kern,
        grid=(batch, steps),
        in_specs=[cspec(wq), cspec(u), cspec(ke), cspec(qk), cspec(gl),
                  pl.BlockSpec((rows, width), lambda b, t: (b * steps + t, zb)),
                  pl.BlockSpec((1, GDN_DIM), lambda b, t: (0, 0))],
        out_specs=pl.BlockSpec((rows, width), lambda b, t: (b * steps + t, 0)),
        out_shape=jax.ShapeDtypeStruct((m, width), BF16),
        scratch_shapes=[pltpu.VMEM((GDN_HEADS, GDN_DIM, GDN_DIM), F32)],
        compiler_params=_cparams(("parallel", "arbitrary")),
        name="gdn_scan",
    )(wq, u, ke, qk, gl, proj, norm_w)


def _outproj_kernel(x_ref, da_ref, gdn_ref, wa_ref, wg_ref, nw_ref, o_ref):
    mix = _dot(da_ref[...], wa_ref[...]) + _dot(gdn_ref[...], wg_ref[...])
    ms = jnp.mean(mix * mix, axis=-1, keepdims=True)
    o_ref[...] = x_ref[...] + mix * lax.rsqrt(ms + EPS) * nw_ref[...]


def _outproj(x, da, gdn, w_da, w_gdn, norm_w, *, tm=512):
    m, d = x.shape
    ka, kg = da.shape[1], gdn.shape[1]
    return pl.pallas_call(
        _outproj_kernel,
        grid=(m // tm,),
        in_specs=[
            pl.BlockSpec((tm, d), lambda i: (i, 0)),
            pl.BlockSpec((tm, ka), lambda i: (i, 0)),
            pl.BlockSpec((tm, kg), lambda i: (i, 0)),
            pl.BlockSpec((ka, d), lambda i: (0, 0)),
            pl.BlockSpec((kg, d), lambda i: (0, 0)),
            pl.BlockSpec((1, d), lambda i: (0, 0)),
        ],
        out_specs=pl.BlockSpec((tm, d), lambda i: (i, 0)),
        out_shape=jax.ShapeDtypeStruct((m, d), F32),
        compiler_params=_cparams(("parallel",)),
        name="outproj",
    )(x, da, gdn, w_da, w_gdn, norm_w)


def _mlp_kernel(x_ref, nw_pre_ref, wup_ref, wdn_ref, nw_post_ref, o_ref, h_ref, acc_ref):
    f = pl.program_id(1)

    @pl.when(f == 0)
    def _():
        x = x_ref[...]
        ms = jnp.mean(x * x, axis=-1, keepdims=True)
        h_ref[...] = (x * lax.rsqrt(ms + EPS) * nw_pre_ref[...]).astype(BF16)
        acc_ref[...] = jnp.zeros(acc_ref.shape, F32)

    a = jnp.maximum(_dot(h_ref[...], wup_ref[...]), 0.0)
    acc_ref[...] += _dot((a * a).astype(BF16), wdn_ref[...])

    @pl.when(f == pl.num_programs(1) - 1)
    def _():
        y = acc_ref[...]
        ms = jnp.mean(y * y, axis=-1, keepdims=True)
        o_ref[...] = x_ref[...] + y * lax.rsqrt(ms + EPS) * nw_post_ref[...]


def _mlp(x, nw_pre, w_up, w_down, nw_post, *, tm=512, tf=512):
    m, d = x.shape
    dff = w_up.shape[1]
    return pl.pallas_call(
        _mlp_kernel,
        grid=(m // tm, dff // tf),
        in_specs=[
            pl.BlockSpec((tm, d), lambda i, f: (i, 0)),
            pl.BlockSpec((1, d), lambda i, f: (0, 0)),
            pl.BlockSpec((d, tf), lambda i, f: (0, f)),
            pl.BlockSpec((tf, d), lambda i, f: (f, 0)),
            pl.BlockSpec((1, d), lambda i, f: (0, 0)),
        ],
        out_specs=pl.BlockSpec((tm, d), lambda i, f: (i, 0)),
        out_shape=jax.ShapeDtypeStruct((m, d), F32),
        scratch_shapes=[pltpu.VMEM((tm, d), BF16), pltpu.VMEM((tm, d), F32)],
        compiler_params=_cparams(("parallel", "arbitrary")),
        name="mlp",
    )(x, nw_pre, w_up, w_down, nw_post)


def _rope_tables(seq):
    half = DA_HEAD_DIM // 2
    inv_freq = ROPE_THETA ** (-jnp.arange(0, DA_HEAD_DIM, 2, dtype=F32) / DA_HEAD_DIM)
    ang = jnp.arange(seq, dtype=F32)[:, None] * inv_freq[None, :]
    cos = jnp.tile(jnp.cos(ang), (1, LANES // half))
    sin = jnp.tile(jnp.concatenate([-jnp.sin(ang), jnp.sin(ang)], axis=1), (1, LANES // DA_HEAD_DIM))
    return cos, sin


def kernel(x, w_in, conv_w, a_log, dt_bias, gdn_norm_w, lambda_q1, lambda_k1, lambda_q2, lambda_k2,
           da_subln_w, w_out, w_up, w_down, norm_pre_mix, norm_post_mix, norm_pre_mlp, norm_post_mlp):
    batch, seq, d = x.shape
    depth = w_in.shape[0]
    da_qk = 2 * DA_HEADS * DA_HEAD_DIM
    da_v = DA_HEADS * 2 * DA_HEAD_DIM
    gdn_w = GDN_HEADS * GDN_DIM
    gdn_col0 = 2 * da_qk + da_v
    z_col0 = gdn_col0 + 3 * gdn_w
    gate_col0 = z_col0 + gdn_w
    cos, sin = _rope_tables(seq)

    xf = x.reshape(batch * seq, d)
    for l in range(depth):
        lambda_init = 0.8 - 0.6 * math.exp(-0.3 * l)
        w_main = w_in[l][:, :gate_col0].astype(BF16)
        w_gate = jnp.pad(w_in[l][:, gate_col0:], ((0, 0), (0, LANES - 2 * GDN_HEADS))).astype(BF16)
        proj, ba = _inproj(xf, norm_pre_mix[l][None, :], w_main, w_gate, cos, sin, seq=seq)

        lam_vecs = jnp.stack([lambda_q1[l], lambda_k1[l], lambda_q2[l], lambda_k2[l]]).astype(F32)
        da_out = _attention(proj, lam_vecs, da_subln_w[l][None, :].astype(F32),
                            batch=batch, seq=seq, lambda_init=lambda_init)

        pad_gate = (GDN_HEADS, LANES - 2 * GDN_HEADS)
        alog_row = jnp.pad(a_log[l].astype(F32), pad_gate)[None, :]
        dtb_row = jnp.pad(dt_bias[l].astype(F32), pad_gate)[None, :]
        wq, u, ke, qk, gl = _gdn_prep(proj, ba, conv_w[l].astype(F32), alog_row, dtb_row,
                                      seq=seq, col0=gdn_col0)
        gdn_out = _gdn_scan(wq, u, ke, qk, gl, proj, gdn_norm_w[l][None, :].astype(F32),
                            batch=batch, seq=seq, z_col0=z_col0)

        w_o = w_out[l].astype(BF16)
        xf = _outproj(xf, da_out, gdn_out, w_o[:da_v], w_o[da_v:], norm_post_mix[l][None, :])
        xf = _mlp(xf, norm_pre_mlp[l][None, :], w_up[l].astype(BF16), w_down[l].astype(BF16),
                  norm_post_mlp[l][None, :])
    return xf.reshape(batch, seq, d)
```
